```python
import jax, jax.numpy as jnp
from jax import lax
import numpy as np

D_MODEL = 2048
BATCH = 2
SEQ = 4096
DEPTH = 1

EPS = 1e-6
NEG_INF = -1e30

RET_HEADS = 8
RET_QK_DIM = 128
RET_V_DIM = 256
RET_CHUNK = 128
RET_THETA = 10000.0
RET_QK_W = RET_HEADS * RET_QK_DIM
RET_V_W = RET_HEADS * RET_V_DIM

ATT_SEGMENTS = ((128, 1), (512, 4), (2048, 16))
ATT_HEADS_PER_GROUP = 8
ATT_HEAD_DIM = 128
ATT_ROT_DIM = ATT_HEAD_DIM // 4
ATT_THETA = 500000.0
ATT_HEADS = len(ATT_SEGMENTS) * ATT_HEADS_PER_GROUP
ATT_W = ATT_HEADS * ATT_HEAD_DIM
ATT_OUT_W = ATT_HEADS_PER_GROUP * ATT_HEAD_DIM

PEER_HEADS = 8
PEER_N_KEYS = 128
PEER_N_EXPERTS = PEER_N_KEYS * PEER_N_KEYS
PEER_QUERY_DIM = 256
PEER_TOPK = 16
PEER_TOKEN_BLOCK = 128

IN_SPLITS = (RET_QK_W, RET_QK_W, RET_V_W, RET_V_W, ATT_W, ATT_W, ATT_W, D_MODEL, D_MODEL)
IN_WIDTH = sum(IN_SPLITS)

kernel_name = 'hybrid_retention_dilated_attn_peer'


def rms_norm(x, g):
    xf = x.astype(jnp.float32)
    y = xf * lax.rsqrt(jnp.mean(xf * xf, axis=-1, keepdims=True) + EPS)
    return (y * g.astype(jnp.float32)).astype(x.dtype)


def rotary(x, rot_dim, theta):
    S = x.shape[1]
    half = rot_dim // 2
    freqs = theta ** (-jnp.arange(half, dtype=jnp.float32) / half)
    ang = jnp.arange(S, dtype=jnp.float32)[:, None] * freqs[None, :]
    cos = jnp.cos(ang)[None, :, None, :].astype(x.dtype)
    sin = jnp.sin(ang)[None, :, None, :].astype(x.dtype)
    x1 = x[..., :half]
    x2 = x[..., half:rot_dim]
    return jnp.concatenate([x1 * cos - x2 * sin, x2 * cos + x1 * sin, x[..., rot_dim:]], axis=-1)


def retention_direction(q, k, v, log_gamma, include_diag):
    B, S, H, dk = q.shape
    dv = v.shape[-1]
    C = RET_CHUNK
    nc = S // C
    qc = q.reshape(B, nc, C, H, dk).transpose(0, 1, 3, 2, 4)
    kc = k.reshape(B, nc, C, H, dk).transpose(0, 1, 3, 2, 4)
    vc = v.reshape(B, nc, C, H, dv).transpose(0, 1, 3, 2, 4)
    lg = log_gamma.astype(jnp.float32)
    pos = jnp.arange(C, dtype=jnp.float32)
    diff = pos[:, None] - pos[None, :]
    lower = (diff >= 0) if include_diag else (diff > 0)
    inner_decay = jnp.where(lower[None], jnp.exp(lg[:, None, None] * jnp.maximum(diff, 0.0)[None]), 0.0)
    q_decay = jnp.exp(lg[:, None] * (pos + 1.0)[None, :])
    k_decay = jnp.exp(lg[:, None] * (C - 1.0 - pos)[None, :])
    chunk_decay = jnp.exp(lg * C)
    scores = jnp.einsum('bnhid,bnhjd->bnhij', qc, kc) * inner_decay
    inner = jnp.einsum('bnhij,bnhje->bnhie', scores, vc)
    updates = jnp.einsum('bnhjd,hj,bnhje->bnhde', kc, k_decay, vc)

    def step(state, upd):
        return chunk_decay[None, :, None, None] * state + upd, state

    init = jnp.zeros((B, H, dk, dv), updates.dtype)
    _, prev = lax.scan(step, init, jnp.moveaxis(updates, 1, 0))
    prev = jnp.moveaxis(prev, 0, 1)
    cross = jnp.einsum('bnhid,bnhde->bnhie', qc, prev) * q_decay[None, None, :, :, None]
    out = inner + cross
    return out.transpose(0, 1, 3, 2, 4).reshape(B, S, H, dv)


def retention_branch(rq, rk, rv, rg, decay_fwd, decay_bwd, gn_g):
    B, S, _ = rq.shape
    q = rotary(rq.reshape(B, S, RET_HEADS, RET_QK_DIM), RET_QK_DIM, RET_THETA)
    k = rotary(rk.reshape(B, S, RET_HEADS, RET_QK_DIM), RET_QK_DIM, RET_THETA) * (RET_QK_DIM ** -0.5)
    v = rv.reshape(B, S, RET_HEADS, RET_V_DIM)
    lg_f = -jnp.exp(decay_fwd.astype(jnp.float32))
    lg_b = -jnp.exp(decay_bwd.astype(jnp.float32))
    fwd = retention_direction(q, k, v, lg_f, True)
    bwd = jnp.flip(retention_direction(jnp.flip(q, 1), jnp.flip(k, 1), jnp.flip(v, 1), lg_b, False), 1)
    o = (fwd + bwd).astype(jnp.float32)
    mu = jnp.mean(o, axis=-1, keepdims=True)
    var = jnp.mean(jnp.square(o - mu), axis=-1, keepdims=True)
    o = ((o - mu) * lax.rsqrt(var + EPS)).reshape(B, S, RET_V_W) * gn_g.astype(jnp.float32)
    return (jax.nn.silu(rg.astype(jnp.float32)) * o).astype(rg.dtype)


def dilated_window_attention(q, k, v, dilation, half):
    B, S, H, d = q.shape
    L = S // dilation
    N = B * dilation

    def by_residue(t):
        return t.reshape(B, L, dilation, H, d).transpose(0, 2, 1, 3, 4).reshape(N, L, H, d)

    qs, ks, vs = by_residue(q), by_residue(k), by_residue(v)
    blk = half
    nb = -(-L // blk)
    Lp = nb * blk
    qb = jnp.pad(qs, ((0, 0), (0, Lp - L), (0, 0), (0, 0))).reshape(N, nb, blk, H, d)
    kv_pad = ((0, 0), (blk, Lp - L + blk), (0, 0), (0, 0))
    kp = jnp.pad(ks, kv_pad).reshape(N, nb + 2, blk, H, d)
    vp = jnp.pad(vs, kv_pad).reshape(N, nb + 2, blk, H, d)
    kb = jnp.concatenate([kp[:, :-2], kp[:, 1:-1], kp[:, 2:]], axis=2)
    vb = jnp.concatenate([vp[:, :-2], vp[:, 1:-1], vp[:, 2:]], axis=2)
    s = jnp.einsum('nbihd,nbjhd->nbhij', qb, kb).astype(jnp.float32) * (d ** -0.5)
    qpos = jnp.arange(nb)[:, None] * blk + jnp.arange(blk)[None, :]
    kpos = jnp.arange(nb)[:, None] * blk - blk + jnp.arange(3 * blk)[None, :]
    rel = kpos[:, None, :] - qpos[:, :, None]
    valid = (jnp.abs(rel) <= half) & (kpos[:, None, :] >= 0) & (kpos[:, None, :] < L)
    s = jnp.where(valid[None, :, None], s, NEG_INF)
    m = jnp.max(s, axis=-1, keepdims=True)
    p = jnp.exp(s - m)
    l = jnp.sum(p, axis=-1, keepdims=True)
    o = jnp.einsum('nbhij,nbjhd->nbihd', p / l, vb)
    lse = (m + jnp.log(l))[..., 0].transpose(0, 1, 3, 2)
    o = o.reshape(N, Lp, H, d)[:, :L].reshape(B, dilation, L, H, d).transpose(0, 2, 1, 3, 4).reshape(B, S, H, d)
    lse = lse.reshape(N, Lp, H)[:, :L].reshape(B, dilation, L, H).transpose(0, 2, 1, 3).reshape(B, S, H)
    return o, lse


def dilated_attention_branch(aq, ak, av, q_norm_g, k_norm_g):
    B, S, _ = aq.shape
    shape = (B, S, ATT_HEADS, ATT_HEAD_DIM)
    q = rotary(rms_norm(aq.reshape(shape), q_norm_g), ATT_ROT_DIM, ATT_THETA)
    k = rotary(rms_norm(ak.reshape(shape), k_norm_g), ATT_ROT_DIM, ATT_THETA)
    v = av.reshape(shape)
    outs, lses = [], []
    for gi, (window, dilation) in enumerate(ATT_SEGMENTS):
        hs = slice(gi * ATT_HEADS_PER_GROUP, (gi + 1) * ATT_HEADS_PER_GROUP)
        o, lse = dilated_window_attention(q[:, :, hs], k[:, :, hs], v[:, :, hs], dilation, window // (2 * dilation))
        outs.append(o)
        lses.append(lse)
    w = jax.nn.softmax(jnp.stack(lses), axis=0)
    o = jnp.einsum('gbsh,gbshd->bshd', w, jnp.stack(outs))
    return o.reshape(B, S, ATT_OUT_W).astype(aq.dtype)


def peer_ffn(h, w_query, sub_keys, u, v):
    B, S, D = h.shape
    hq = PEER_QUERY_DIM // 2
    q = (h @ w_query).reshape(B, S, PEER_HEADS, PEER_QUERY_DIM).astype(jnp.float32)
    s1 = jnp.einsum('bshd,hkd->bshk', q[..., :hq], sub_keys[:, 0].astype(jnp.float32))
    s2 = jnp.einsum('bshd,hkd->bshk', q[..., hq:], sub_keys[:, 1].astype(jnp.float32))
    v1, i1 = lax.top_k(s1, PEER_TOPK)
    v2, i2 = lax.top_k(s2, PEER_TOPK)
    cand = (v1[..., :, None] + v2[..., None, :]).reshape(B, S, PEER_HEADS, PEER_TOPK * PEER_TOPK)
    best, pos = lax.top_k(cand, PEER_TOPK)
    e1 = jnp.take_along_axis(i1, pos // PEER_TOPK, axis=-1)
    e2 = jnp.take_along_axis(i2, pos % PEER_TOPK, axis=-1)
    experts = e1 * PEER_N_KEYS + e2
    gates = jax.nn.softmax(best, axis=-1)
    T = B * S
    nblk = T // PEER_TOKEN_BLOCK

    def block(args):
        hb, eb, gb = args
        a = jnp.einsum('td,thkd->thk', hb, u[eb])
        act = jax.nn.gelu(a.astype(jnp.float32), approximate=False) * gb
        return jnp.einsum('thk,thkd->td', act.astype(hb.dtype), v[eb])

    out = lax.map(block, (h.reshape(nblk, PEER_TOKEN_BLOCK, D),
                          experts.reshape(nblk, PEER_TOKEN_BLOCK, PEER_HEADS, PEER_TOPK),
                          gates.reshape(nblk, PEER_TOKEN_BLOCK, PEER_HEADS, PEER_TOPK)))
    return out.reshape(B, S, D)


def setup_inputs(seed: int = 0) -> dict:
    key = jax.random.key(seed)
    ks = jax.random.split(key, 16)
    f32 = jnp.float32

    def normal(k, shape, scale):
        return jax.random.normal(k, shape, f32) * scale

    heads = jnp.arange(RET_HEADS, dtype=f32)
    base = jnp.log(-jnp.log1p(-jnp.exp2(-5.0 - heads)))
    return {
        'x': normal(ks[0], (BATCH, SEQ, D_MODEL), 1.0),
        'mix_norm_g': 1.0 + normal(ks[1], (DEPTH, D_MODEL), 0.02),
        'w_in': normal(ks[2], (DEPTH, D_MODEL, IN_WIDTH), D_MODEL ** -0.5),
        'ret_decay_fwd': base + normal(ks[3], (DEPTH, RET_HEADS), 0.01),
        'ret_decay_bwd': base + normal(ks[4], (DEPTH, RET_HEADS), 0.01),
        'ret_gn_g': 1.0 + normal(ks[5], (DEPTH, RET_V_W), 0.02),
        'w_ret_out': normal(ks[6], (DEPTH, RET_V_W, D_MODEL), RET_V_W ** -0.5),
        'attn_q_norm_g': 1.0 + normal(ks[7], (DEPTH, ATT_HEAD_DIM), 0.02),
        'attn_k_norm_g': 1.0 + normal(ks[8], (DEPTH, ATT_HEAD_DIM), 0.02),
        'w_attn_out': normal(ks[9], (DEPTH, ATT_OUT_W, D_MODEL), ATT_OUT_W ** -0.5),
        'w_out': normal(ks[10], (DEPTH, D_MODEL, D_MODEL), D_MODEL ** -0.5),
        'ffn_norm_g': 1.0 + normal(ks[11], (DEPTH, D_MODEL), 0.02),
        'peer_w_query': normal(ks[12], (DEPTH, D_MODEL, PEER_HEADS * PEER_QUERY_DIM), D_MODEL ** -0.5),
        'peer_sub_keys': normal(ks[13], (DEPTH, PEER_HEADS, 2, PEER_N_KEYS, PEER_QUERY_DIM // 2), (PEER_QUERY_DIM // 2) ** -0.5),
        'peer_u': normal(ks[14], (DEPTH, PEER_N_EXPERTS, D_MODEL), D_MODEL ** -0.5),
        'peer_v': normal(ks[15], (DEPTH, PEER_N_EXPERTS, D_MODEL), PEER_TOPK ** -0.5),
    }


def reference(x, mix_norm_g, w_in, ret_decay_fwd, ret_decay_bwd, ret_gn_g, w_ret_out,
              attn_q_norm_g, attn_k_norm_g, w_attn_out, w_out, ffn_norm_g,
              peer_w_query, peer_sub_keys, peer_u, peer_v):
    split_at = np.cumsum(IN_SPLITS)[:-1].tolist()
    for l in range(DEPTH):
        h = rms_norm(x, mix_norm_g[l])
        proj = h @ w_in[l]
        rq, rk, rv, rg, aq, ak, av, gate_ret, gate_att = jnp.split(proj, split_at, axis=-1)
        ret = retention_branch(rq, rk, rv, rg, ret_decay_fwd[l], ret_decay_bwd[l], ret_gn_g[l]) @ w_ret_out[l]
        att = dilated_attention_branch(aq, ak, av, attn_q_norm_g[l], attn_k_norm_g[l]) @ w_attn_out[l]
        merged = jax.nn.sigmoid(gate_ret) * ret + jax.nn.sigmoid(gate_att) * att
        x = x + merged @ w_out[l]
        h2 = rms_norm(x, ffn_norm_g[l])
        x = x + peer_ffn(h2, peer_w_query[l], peer_sub_keys[l], peer_u[l], peer_v[l])
    return x
```

```python
import functools

import jax
import jax.numpy as jnp
from jax import lax
from jax.experimental import pallas as pl
from jax.experimental.pallas import tpu as pltpu

F32 = jnp.float32
BF16 = jnp.bfloat16

EPS = 1e-6
NEG_INF = -1e30

RET_HEADS = 8
RET_QK_DIM = 128
RET_V_DIM = 256
RET_CHUNK = 128
RET_THETA = 10000.0
RET_QK_W = RET_HEADS * RET_QK_DIM
RET_V_W = RET_HEADS * RET_V_DIM

ATT_SEGMENTS = ((128, 1), (512, 4), (2048, 16))
ATT_HEADS_PER_GROUP = 8
ATT_HEAD_DIM = 128
ATT_ROT_DIM = ATT_HEAD_DIM // 4
ATT_THETA = 500000.0
ATT_HEADS = len(ATT_SEGMENTS) * ATT_HEADS_PER_GROUP
ATT_W = ATT_HEADS * ATT_HEAD_DIM
ATT_OUT_W = ATT_HEADS_PER_GROUP * ATT_HEAD_DIM
ATT_Q_BLOCK = 128
ATT_K_WINDOW = 256

PEER_HEADS = 8
PEER_N_KEYS = 128
PEER_QUERY_DIM = 256
PEER_TOPK = 16

LANES = 128
V7X_VMEM_LIMIT_BYTES = 60000 * 1024


def _cparams(semantics):
    return pltpu.CompilerParams(dimension_semantics=semantics,
                                vmem_limit_bytes=V7X_VMEM_LIMIT_BYTES)


def _norm_matmul_kernel(x_ref, g_ref, w_ref, o_ref, h_ref):
    @pl.when(pl.program_id(1) == 0)
    def _():
        x = x_ref[...]
        ms = jnp.mean(x * x, axis=-1, keepdims=True)
        h_ref[...] = (x * lax.rsqrt(ms + EPS) * g_ref[...]).astype(h_ref.dtype)

    o_ref[...] = jnp.dot(h_ref[...], w_ref[...],
                         preferred_element_type=F32).astype(o_ref.dtype)


def _norm_matmul(x, g, w, out_dtype, *, emit_h, tm=1024, tn=1024, name):
    T, D = x.shape
    N = w.shape[1]
    grid = (T // tm, N // tn)
    in_specs = [
        pl.BlockSpec((tm, D), lambda i, j: (i, 0)),
        pl.BlockSpec((1, D), lambda i, j: (0, 0)),
        pl.BlockSpec((D, tn), lambda i, j: (0, j)),
    ]
    o_spec = pl.BlockSpec((tm, tn), lambda i, j: (i, j))
    o_shape = jax.ShapeDtypeStruct((T, N), out_dtype)
    if emit_h:
        out_specs = [o_spec, pl.BlockSpec((tm, D), lambda i, j: (i, 0))]
        out_shape = [o_shape, jax.ShapeDtypeStruct((T, D), BF16)]
        scratch = []
    else:
        out_specs = o_spec
        out_shape = o_shape
        scratch = [pltpu.VMEM((tm, D), BF16)]
    return pl.pallas_call(
        _norm_matmul_kernel,
        out_shape=out_shape,
        grid=grid,
        in_specs=in_specs,
        out_specs=out_specs,
        scratch_shapes=scratch,
        compiler_params=_cparams(("parallel", "arbitrary")),
        name=name,
    )(x, g, w)


def _retention_kernel(dec_ref, q_ref, k_ref, v_ref, rg_ref, cos_ref, sin_ref, gn_ref,
                      o_ref, qr_s, kr_s, cf_s, *, seq):
    C = RET_CHUNK
    nc = seq // C
    dk, dv = RET_QK_DIM, RET_V_DIM
    h = pl.program_id(1)

    lgf_k = -jnp.exp(jnp.full((1, dk), dec_ref[0, h], F32))
    lgb_k = -jnp.exp(jnp.full((1, dk), dec_ref[1, h], F32))
    lgf_v = -jnp.exp(jnp.full((1, dv), dec_ref[0, h], F32))
    lgb_v = -jnp.exp(jnp.full((1, dv), dec_ref[1, h], F32))

    ri = lax.broadcasted_iota(jnp.int32, (C, C), 0).astype(F32)
    ci = lax.broadcasted_iota(jnp.int32, (C, C), 1).astype(F32)
    diff = ri - ci
    dmat = jnp.where(diff >= 0.0,
                     jnp.exp(lgf_k * jnp.maximum(diff, 0.0)),
                     jnp.exp(lgb_k * jnp.maximum(-diff, 0.0)))
    row_k = lax.broadcasted_iota(jnp.int32, (C, dk), 0).astype(F32)
    row_v = lax.broadcasted_iota(jnp.int32, (C, dv), 0).astype(F32)
    xi_f = jnp.exp(lgf_v * (row_v + 1.0))
    xi_b = jnp.exp(lgb_v * (C - row_v))
    zeta_f = jnp.exp(lgf_k * (C - 1.0 - row_k))
    zeta_b = jnp.exp(lgb_k * row_k)
    gc_f = jnp.exp(lgf_v * float(C))
    gc_b = jnp.exp(lgb_v * float(C))
    k_scale = RET_QK_DIM ** -0.5

    def rows(n):
        return pl.ds(pl.multiple_of(n * C, C), C)

    def rot_body(n, carry):
        sl = rows(n)
        cs = cos_ref[sl, :]
        sn = sin_ref[sl, :]
        xq = q_ref[sl, :].astype(F32)
        xk = k_ref[sl, :].astype(F32)
        qr_s[sl, :] = (xq * cs + pltpu.roll(xq, dk // 2, 1) * sn).astype(BF16)
        kr_s[sl, :] = ((xk * cs + pltpu.roll(xk, dk // 2, 1) * sn) * k_scale).astype(BF16)
        return carry

    lax.fori_loop(0, nc, rot_body, 0)

    def state_update(state, kc, vc, zeta, gc):
        kz = (kc.astype(F32) * zeta).T.astype(BF16)
        return gc * state + jnp.dot(kz, vc, preferred_element_type=F32)

    def fwd_body(n, state):
        sl = rows(n)
        qc = qr_s[sl, :]
        cf_s[sl, :] = jnp.dot(qc, state.astype(BF16), preferred_element_type=F32) * xi_f
        return state_update(state, kr_s[sl, :], v_ref[sl, :], zeta_f, gc_f)

    lax.fori_loop(0, nc, fwd_body, jnp.zeros((dk, dv), F32))

    gn = gn_ref[...]

    def bwd_body(i, state):
        n = nc - 1 - i
        sl = rows(n)
        qc = qr_s[sl, :]
        kc = kr_s[sl, :]
        vc = v_ref[sl, :]
        s = lax.dot_general(qc, kc, (((1,), (1,)), ((), ())), preferred_element_type=F32)
        inner = jnp.dot((s * dmat).astype(BF16), vc, preferred_element_type=F32)
        cross_b = jnp.dot(qc, state.astype(BF16), preferred_element_type=F32) * xi_b
        o = (inner + cf_s[sl, :]) + cross_b
        mu = jnp.mean(o, axis=-1, keepdims=True)
        d = o - mu
        var = jnp.mean(d * d, axis=-1, keepdims=True)
        y = d * lax.rsqrt(var + EPS) * gn
        gate = rg_ref[sl, :].astype(F32)
        o_ref[sl, :] = (gate * jax.nn.sigmoid(gate) * y).astype(o_ref.dtype)
        return state_update(state, kc, vc, zeta_b, gc_b)

    lax.fori_loop(0, nc, bwd_body, jnp.zeros((dk, dv), F32))


def _retention(proj_ret, decays, cos_t, sin_t, gn_g, *, batch, seq):
    T = batch * seq
    dk, dv = RET_QK_DIM, RET_V_DIM
    kq = RET_QK_W // dk
    kv = 2 * RET_QK_W // dv
    kg = kv + RET_V_W // dv
    return pl.pallas_call(
        functools.partial(_retention_kernel, seq=seq),
        out_shape=jax.ShapeDtypeStruct((T, RET_V_W), BF16),
        grid=(batch, RET_HEADS),
        in_specs=[
            pl.BlockSpec(memory_space=pltpu.SMEM),
            pl.BlockSpec((seq, dk), lambda b, h: (b, h)),
            pl.BlockSpec((seq, dk), lambda b, h: (b, kq + h)),
            pl.BlockSpec((seq, dv), lambda b, h: (b, kv + h)),
            pl.BlockSpec((seq, dv), lambda b, h: (b, kg + h)),
            pl.BlockSpec((seq, dk), lambda b, h: (0, 0)),
            pl.BlockSpec((seq, dk), lambda b, h: (0, 0)),
            pl.BlockSpec((1, dv), lambda b, h: (0, h)),
        ],
        out_specs=pl.BlockSpec((seq, dv), lambda b, h: (b, h)),
        scratch_shapes=[
            pltpu.VMEM((seq, dk), BF16),
            pltpu.VMEM((seq, dk), BF16),
            pltpu.VMEM((seq, dv), F32),
        ],
        compiler_params=_cparams(("parallel", "parallel")),
        name="retention",
    )(decays, proj_ret, proj_ret, proj_ret, proj_ret, cos_t, sin_t, gn_g)


def _dilated_attn_kernel(q_ref, k_ref, v_ref, tab_ref, qg_ref, kg_ref, o_ref, lse_ref,
                         qn_s, kn_s, vb_s, *, seq, dilation, half):
    r = dilation
    L = seq // r
    TQ, KW = ATT_Q_BLOCK, ATT_K_WINDOW
    d = ATT_HEAD_DIM
    hr = ATT_ROT_DIM // 2
    nqb = L // TQ
    scale = d ** -0.5
    lane = lax.broadcasted_iota(jnp.int32, (TQ, d), 1)
    qg = qg_ref[...]
    kg = kg_ref[...]

    def strided_rows(p, t0):
        if r == 1:
            return pl.ds(pl.multiple_of(t0, TQ), TQ)
        return pl.ds(p + r * t0, TQ, stride=r)

    def prep_body(idx, carry):
        p = idx // nqb
        t0 = (idx % nqb) * TQ
        src = strided_rows(p, t0)
        dst = pl.ds(pl.multiple_of(p * L + t0, TQ), TQ)
        tab = tab_ref[src, :]
        cosv = jnp.where(lane < 2 * hr, tab, 1.0)
        sinv = jnp.where(lane < 2 * hr, pltpu.roll(tab, d - 2 * hr, 1), 0.0)

        def norm_rot(x_ref, gain):
            x = x_ref[src, :]
            y = x * lax.rsqrt(jnp.mean(x * x, axis=-1, keepdims=True) + EPS) * gain
            partner = jnp.where(lane < hr, pltpu.roll(y, d - hr, 1), pltpu.roll(y, hr, 1))
            return (y * cosv + partner * sinv).astype(BF16)

        qn_s[dst, :] = norm_rot(q_ref, qg)
        kn_s[dst, :] = norm_rot(k_ref, kg)
        vb_s[dst, :] = v_ref[src, :].astype(BF16)
        return carry

    lax.fori_loop(0, r * nqb, prep_body, 0)

    rel = (lax.broadcasted_iota(jnp.int32, (TQ, KW), 1)
           - lax.broadcasted_iota(jnp.int32, (TQ, KW), 0))

    def attn_body(idx, carry):
        p = idx // nqb
        t0 = (idx % nqb) * TQ
        ks = jnp.clip(t0 - half, 0, L - KW)
        base = p * L
        q = qn_s[pl.ds(pl.multiple_of(base + t0, TQ), TQ), :]
        kwin = pl.ds(pl.multiple_of(base + ks, half), KW)
        k = kn_s[kwin, :]
        v = vb_s[kwin, :]
        s = lax.dot_general(q, k, (((1,), (1,)), ((), ())), preferred_element_type=F32) * scale
        s = jnp.where(jnp.abs(rel + (ks - t0)) <= half, s, NEG_INF)
        m = jnp.max(s, axis=-1, keepdims=True)
        pe = jnp.exp(s - m)
        l = jnp.sum(pe, axis=-1, keepdims=True)
        acc = jnp.dot(pe.astype(BF16), v, preferred_element_type=F32)
        dst = strided_rows(p, t0)
        o_ref[dst, :] = acc / l
        lse_ref[dst, :] = jnp.broadcast_to(m + jnp.log(l), (TQ, d))
        return carry

    lax.fori_loop(0, r * nqb, attn_body, 0)


def _dilated_attn(proj_att, tab, q_g, k_g, *, batch, seq, group, dilation, half):
    T = batch * seq
    d = ATT_HEAD_DIM
    hg = ATT_HEADS_PER_GROUP
    qo = group * hg
    ko = ATT_HEADS + group * hg
    vo = 2 * ATT_HEADS + group * hg
    assert seq // dilation >= ATT_K_WINDOW and ATT_K_WINDOW == ATT_Q_BLOCK + 2 * half
    out = jax.ShapeDtypeStruct((T, hg * d), F32)
    return pl.pallas_call(
        functools.partial(_dilated_attn_kernel, seq=seq, dilation=dilation, half=half),
        out_shape=[out, out],
        grid=(batch, hg),
        in_specs=[
            pl.BlockSpec((seq, d), lambda b, j: (b, qo + j)),
            pl.BlockSpec((seq, d), lambda b, j: (b, ko + j)),
            pl.BlockSpec((seq, d), lambda b, j: (b, vo + j)),
            pl.BlockSpec((seq, d), lambda b, j: (0, 0)),
            pl.BlockSpec((1, d), lambda b, j: (0, 0)),
            pl.BlockSpec((1, d), lambda b, j: (0, 0)),
        ],
        out_specs=[pl.BlockSpec((seq, d), lambda b, j: (b, j)),
                   pl.BlockSpec((seq, d), lambda b, j: (b, j))],
        scratch_shapes=[pltpu.VMEM((seq, d), BF16)] * 3,
        compiler_params=_cparams(("parallel", "parallel")),
        name=f"dilated_attn_r{dilation}",
    )(proj_att, proj_att, proj_att, tab, q_g, k_g)


def _merge_kernel(o0, o1, o2, l0, l1, l2, a_ref):
    a, b, c = l0[...], l1[...], l2[...]
    m = jnp.maximum(jnp.maximum(a, b), c)
    ea, eb, ec = jnp.exp(a - m), jnp.exp(b - m), jnp.exp(c - m)
    num = ea * o0[...] + eb * o1[...] + ec * o2[...]
    a_ref[...] = (num / (ea + eb + ec)).astype(a_ref.dtype)


def _merge(outs, lses, *, tm=512):
    T, W = outs[0].shape
    spec = pl.BlockSpec((tm, W), lambda i: (i, 0))
    return pl.pallas_call(
        _merge_kernel,
        out_shape=jax.ShapeDtypeStruct((T, W), BF16),
        grid=(T // tm,),
        in_specs=[spec] * 6,
        out_specs=spec,
        compiler_params=_cparams(("parallel",)),
        name="attn_merge",
    )(*outs, *lses)


def _gated_proj_kernel(r_ref, a_ref, gr_ref, ga_ref, wr_ref, wa_ref, o_ref):
    ret = jnp.dot(r_ref[...], wr_ref[...], preferred_element_type=F32)
    att = jnp.dot(a_ref[...], wa_ref[...], preferred_element_type=F32)
    gr = jax.nn.sigmoid(gr_ref[...].astype(F32))
    ga = jax.nn.sigmoid(ga_ref[...].astype(F32))
    o_ref[...] = (gr * ret + ga * att).astype(o_ref.dtype)


def _gated_proj(r_act, a_act, gates, w_r, w_a, *, tm=1024, tn=1024):
    T, Kr = r_act.shape
    Ka = a_act.shape[1]
    N = w_r.shape[1]
    nj = N // tn
    return pl.pallas_call(
        _gated_proj_kernel,
        out_shape=jax.ShapeDtypeStruct((T, N), BF16),
        grid=(T // tm, nj),
        in_specs=[
            pl.BlockSpec((tm, Kr), lambda i, j: (i, 0)),
            pl.BlockSpec((tm, Ka), lambda i, j: (i, 0)),
            pl.BlockSpec((tm, tn), lambda i, j: (i, j)),
            pl.BlockSpec((tm, tn), lambda i, j: (i, nj + j)),
            pl.BlockSpec((Kr, tn), lambda i, j: (0, j)),
            pl.BlockSpec((Ka, tn), lambda i, j: (0, j)),
        ],
        out_specs=pl.BlockSpec((tm, tn), lambda i, j: (i, j)),
        compiler_params=_cparams(("parallel", "arbitrary")),
        name="gated_proj",
    )(r_act, a_act, gates, gates, w_r, w_a)


def _residual_matmul_kernel(x_ref, a_ref, w_ref, o_ref):
    o_ref[...] = x_ref[...] + jnp.dot(a_ref[...], w_ref[...], preferred_element_type=F32)


def _residual_matmul(x, a, w, *, tm=1024, tn=1024):
    T, K = a.shape
    N = w.shape[1]
    return pl.pallas_call(
        _residual_matmul_kernel,
        out_shape=jax.ShapeDtypeStruct((T, N), F32),
        grid=(T // tm, N // tn),
        in_specs=[
            pl.BlockSpec((tm, tn), lambda i, j: (i, j)),
            pl.BlockSpec((tm, K), lambda i, j: (i, 0)),
            pl.BlockSpec((K, tn), lambda i, j: (0, j)),
        ],
        out_specs=pl.BlockSpec((tm, tn), lambda i, j: (i, j)),
        compiler_params=_cparams(("parallel", "arbitrary")),
        name="residual_matmul",
    )(x, a, w)


def _top_values(x, k):
    n, t = x.shape
    row = lax.broadcasted_iota(jnp.int32, (k, t), 0)
    vals = jnp.zeros((k, t), F32)
    work = x
    for i in range(k):
        mx = jnp.max(work, axis=0, keepdims=True)
        vals = jnp.where(row == i, mx, vals)
        work = jnp.where(work == mx, -jnp.inf, work)
    return vals


def _peer_route_kernel(q_ref, keys_ref, s2_ref, e2_ref, th_ref, c_ref):
    hq = PEER_QUERY_DIM // 2
    K = PEER_TOPK
    nt = (((1,), (1,)), ((), ()))
    for h in range(PEER_HEADS):
        qa = q_ref[:, h * PEER_QUERY_DIM: h * PEER_QUERY_DIM + hq]
        qb = q_ref[:, h * PEER_QUERY_DIM + hq: (h + 1) * PEER_QUERY_DIM]
        s1 = lax.dot_general(keys_ref[h, 0], qa, nt, preferred_element_type=F32)
        s2 = lax.dot_general(keys_ref[h, 1], qb, nt, preferred_element_type=F32)
        v1 = _top_values(s1, K)
        v2 = _top_values(s2, K)
        cand = jnp.concatenate([v1[a:a + 1, :] + v2 for a in range(K)], axis=0)
        best = _top_values(cand, K)
        tau = best[K - 1:K, :]
        z = jnp.sum(jnp.exp(best - best[0:1, :]), axis=0, keepdims=True)
        th = jnp.full(s1.shape, jnp.inf, F32)
        for a in range(K):
            sums = cand[a * K:(a + 1) * K, :]
            theta = jnp.min(jnp.where(sums >= tau, v2, jnp.inf), axis=0, keepdims=True)
            th = jnp.where(s1 == v1[a:a + 1, :], theta, th)
        s2_ref[h] = s2
        e2_ref[h] = jnp.exp(s2 - v2[0:1, :])
        th_ref[h] = th
        c_ref[h] = jnp.exp(s1 - v1[0:1, :]) / z


def _peer_route(q, keys, *, tm=256):
    T = q.shape[0]
    tab = jax.ShapeDtypeStruct((PEER_HEADS, PEER_N_KEYS, T), F32)
    tab_spec = pl.BlockSpec((PEER_HEADS, PEER_N_KEYS, tm), lambda i: (0, 0, i))
    return pl.pallas_call(
        _peer_route_kernel,
        out_shape=[tab] * 4,
        grid=(T // tm,),
        in_specs=[
            pl.BlockSpec((tm, PEER_HEADS * PEER_QUERY_DIM), lambda i: (i, 0)),
            pl.BlockSpec(keys.shape, lambda i: (0, 0, 0, 0)),
        ],
        out_specs=[tab_spec] * 4,
        compiler_params=_cparams(("parallel",)),
        name="peer_route",
    )(q, keys)


PEER_E1_STEP = 8


def _peer_experts_kernel(x_ref, h_ref, u_ref, v_ref, s2_ref, e2_ref, th_ref, c_ref,
                         o_ref, w_s):
    c_step = pl.program_id(1)
    nk = PEER_N_KEYS
    tm = h_ref.shape[0]

    @pl.when(c_step == 0)
    def _():
        o_ref[...] = x_ref[...]

    a_t = lax.dot_general(u_ref[...], h_ref[...], (((1,), (1,)), ((), ())),
                          preferred_element_type=F32)
    for e in range(PEER_E1_STEP):
        for lb in range(tm // LANES):
            ls = slice(lb * LANES, (lb + 1) * LANES)
            g = jnp.zeros((nk, LANES), F32)
            for h in range(PEER_HEADS):
                th = th_ref[h, e:e + 1, ls]
                cc = c_ref[h, e:e + 1, ls]
                g = g + jnp.where(s2_ref[h, :, ls] >= th, e2_ref[h, :, ls], 0.0) * cc
            a = a_t[e * nk:(e + 1) * nk, ls]
            act = 0.5 * a * (1.0 + lax.erf(a * (2.0 ** -0.5)))
            w_s[e * nk:(e + 1) * nk, ls] = (act * g).astype(w_s.dtype)

    o_ref[...] += lax.dot_general(w_s[...], v_ref[...], (((0,), (0,)), ((), ())),
                                  preferred_element_type=F32)


def _peer_experts(x, h, u, v, s2, e2, th, c, *, tm=512):
    T, D = x.shape
    n_exp = u.shape[0]
    te = PEER_E1_STEP * PEER_N_KEYS
    full = pl.BlockSpec((PEER_HEADS, PEER_N_KEYS, tm), lambda i, j: (0, 0, i))
    step = pl.BlockSpec((PEER_HEADS, PEER_E1_STEP, tm), lambda i, j: (0, j, i))
    return pl.pallas_call(
        _peer_experts_kernel,
        out_shape=jax.ShapeDtypeStruct((T, D), F32),
        grid=(T // tm, n_exp // te),
        in_specs=[
            pl.BlockSpec((tm, D), lambda i, j: (i, 0)),
            pl.BlockSpec((tm, D), lambda i, j: (i, 0)),
            pl.BlockSpec((te, D), lambda i, j: (j, 0)),
            pl.BlockSpec((te, D), lambda i, j: (j, 0)),
            full, full, step, step,
        ],
        out_specs=pl.BlockSpec((tm, D), lambda i, j: (i, 0)),
        scratch_shapes=[pltpu.VMEM((te, tm), BF16)],
        compiler_params=_cparams(("parallel", "arbitrary")),
        name="peer_experts",
    )(x, h, u, v, s2, e2, th, c)


def _rotary_angles(seq, half, theta):
    freqs = theta ** (-jnp.arange(half, dtype=F32) / half)
    return jnp.arange(seq, dtype=F32)[:, None] * freqs[None, :]


def _retention_tables(seq):
    ang = _rotary_angles(seq, RET_QK_DIM // 2, RET_THETA)
    cos, sin = jnp.cos(ang), jnp.sin(ang)
    return jnp.concatenate([cos, cos], axis=1), jnp.concatenate([-sin, sin], axis=1)


def _attention_table(seq):
    ang = _rotary_angles(seq, ATT_ROT_DIM // 2, ATT_THETA)
    cos, sin = jnp.cos(ang), jnp.sin(ang)
    pad = jnp.zeros((seq, ATT_HEAD_DIM - 2 * ATT_ROT_DIM), F32)
    return jnp.concatenate([cos, cos, -sin, sin, pad], axis=1)


def kernel(x, mix_norm_g, w_in, ret_decay_fwd, ret_decay_bwd, ret_gn_g, w_ret_out, attn_q_norm_g, attn_k_norm_g, w_attn_out, w_out, ffn_norm_g, peer_w_query, peer_sub_keys, peer_u, peer_v):
    B, S, D = x.shape
    T = B * S
    depth = w_in.shape[0]
    ret_w = 2 * RET_QK_W + 2 * RET_V_W
    att_w = 3 * ATT_W

    cos_t, sin_t = _retention_tables(S)
    att_tab = _attention_table(S)

    xf = x.reshape(T, D)
    for l in range(depth):
        w_l = w_in[l].astype(BF16)
        g_l = mix_norm_g[l].reshape(1, D)
        proj_ret = _norm_matmul(xf, g_l, w_l[:, :ret_w], BF16, emit_h=False, name="in_proj_ret")
        proj_att = _norm_matmul(xf, g_l, w_l[:, ret_w:ret_w + att_w], F32, emit_h=False,
                                name="in_proj_att")
        gates = _norm_matmul(xf, g_l, w_l[:, ret_w + att_w:], BF16, emit_h=False,
                             name="in_proj_gates")

        decays = jnp.stack([ret_decay_fwd[l], ret_decay_bwd[l]]).astype(F32)
        r_act = _retention(proj_ret, decays, cos_t, sin_t, ret_gn_g[l].reshape(1, RET_V_W),
                           batch=B, seq=S)

        outs, lses = [], []
        for gi, (window, dilation) in enumerate(ATT_SEGMENTS):
            o, lse = _dilated_attn(proj_att, att_tab,
                                   attn_q_norm_g[l].reshape(1, ATT_HEAD_DIM),
                                   attn_k_norm_g[l].reshape(1, ATT_HEAD_DIM),
                                   batch=B, seq=S, group=gi, dilation=dilation,
                                   half=window // (2 * dilation))
            outs.append(o)
            lses.append(lse)
        a_act = _merge(outs, lses)

        merged = _gated_proj(r_act, a_act, gates, w_ret_out[l].astype(BF16),
                             w_attn_out[l].astype(BF16))
        x1 = _residual_matmul(xf, merged, w_out[l].astype(BF16))

        q_peer, h2 = _norm_matmul(x1, ffn_norm_g[l].reshape(1, D), peer_w_query[l].astype(BF16),
                                  BF16, emit_h=True, name="peer_query")
        s2, e2, th, c = _peer_route(q_peer, peer_sub_keys[l].astype(BF16))
        xf = _peer_experts(x1, h2, peer_u[l].astype(BF16), peer_v[l].astype(BF16), s2, e2, th, c)
    return xf.reshape(B, S, D)
```
